```python
import jax, jax.numpy as jnp
from jax import lax
import numpy as np

D_MODEL = 1024
BATCH = 2
SEQ = 8192
DEPTH = 2
DEC_BATCH = 8
DEC_SEQ = 64
PAST_LEN = 1024

CHUNK = 64
HEAD_DIM = 64
MIX_WIDTH = D_MODEL
W_A = MIX_WIDTH // 2
W_B = MIX_WIDTH - W_A
H_A = W_A // HEAD_DIM
H_B = W_B // HEAD_DIM
N_PREV_CHUNKS = 8
BAND_CHUNKS = N_PREV_CHUNKS + 1
BAND_PAST = N_PREV_CHUNKS * CHUNK
REL_CLIP = 128
N_REL = 2 * REL_CLIP + 1
Q_BLOCK = 128
N_GROUPS = 4
EXPERTS_PER_GROUP = 8
N_EXPERTS = N_GROUPS * EXPERTS_PER_GROUP
TOP_K_IN_GROUP = 2
D_EXPERT = D_MODEL // 4
IN_COLS = 3 * W_A + H_A + 3 * W_B
SPLITS = (W_A, 2 * W_A, 3 * W_A, 3 * W_A + H_A, 3 * W_A + H_A + W_B, 3 * W_A + H_A + 2 * W_B)
N_MOD = 6
RMS_EPS = 1e-6
ATTN_SCALE = HEAD_DIM ** -0.5
FORGET_BIAS_INIT = 3.0

kernel_name = 'hymba_fox_chunkband_hmoe_stream_step'


def rmsnorm(x, g):
    xf = x.astype(jnp.float32)
    y = xf * lax.rsqrt(jnp.mean(xf * xf, axis=-1, keepdims=True) + RMS_EPS)
    return (y * g.astype(jnp.float32)).astype(x.dtype)


def head_rmsnorm(o, g):
    n, l, h, d = o.shape
    return rmsnorm(o, g.reshape(h, d)).reshape(n, l, h * d)


def modulation(c, w_ada, b_ada):
    m = jax.nn.silu(c) @ w_ada + b_ada
    return jnp.split(m[:, None, :], N_MOD, axis=-1)


def modulate(x, g, shift, scale):
    return rmsnorm(x, g) * (1 + scale) + shift


def project(h, w_in, b_f):
    n, l, _ = h.shape
    qa, ka, va, fa, qb, kb, vb = jnp.split(h @ w_in, SPLITS, axis=-1)
    heads_a = lambda t: t.reshape(n, l, H_A, HEAD_DIM)
    heads_b = lambda t: t.reshape(n, l, H_B, HEAD_DIM)
    logf = jax.nn.log_sigmoid((fa + b_f).astype(jnp.float32))
    return heads_a(qa), heads_a(ka), heads_a(va), logf, heads_b(qb), heads_b(kb), heads_b(vb)


def merge(o_a, o_b, g_a, g_b, w_out):
    return jnp.concatenate([head_rmsnorm(o_a, g_a), head_rmsnorm(o_b, g_b)], axis=-1) @ w_out


def fox_attend(q, fq, qpos, k, v, fk, kpos):
    s = jnp.einsum('nqhd,nkhd->nhqk', q, k).astype(jnp.float32) * ATTN_SCALE
    s = s + jnp.swapaxes(fq, 1, 2)[:, :, :, None] - jnp.swapaxes(fk, 1, 2)[:, :, None, :]
    s = jnp.where(kpos[None, :] <= qpos[:, None], s, -jnp.inf)
    p = jax.nn.softmax(s, axis=-1).astype(v.dtype)
    return jnp.einsum('nhqk,nkhd->nqhd', p, v)


def fox_prompt(q, k, v, logf):
    n, s, h, d = q.shape
    nb = s // Q_BLOCK
    f = jnp.cumsum(logf, axis=1)
    pos = jnp.arange(s)
    blocks = (q.reshape(n, nb, Q_BLOCK, h, d).swapaxes(0, 1),
              f.reshape(n, nb, Q_BLOCK, h).swapaxes(0, 1),
              pos.reshape(nb, Q_BLOCK))
    o = lax.map(lambda b: fox_attend(b[0], b[1], b[2], k, v, f, pos), blocks)
    return o.swapaxes(0, 1).reshape(n, s, h, d)


def fox_sample(q, k, v, logf, cache_k, cache_v, cache_logf):
    p_len, t = cache_k.shape[1], q.shape[1]
    k_all = jnp.concatenate([cache_k, k], axis=1)
    v_all = jnp.concatenate([cache_v, v], axis=1)
    f = jnp.cumsum(jnp.concatenate([cache_logf.astype(jnp.float32), logf], axis=1), axis=1)
    kpos = jnp.arange(p_len + t)
    return fox_attend(q, f[:, p_len:], kpos[p_len:], k_all, v_all, f, kpos)


def band_mask(qpos, kpos):
    qc = (qpos // CHUNK)[:, :, None]
    kc = (kpos // CHUNK)[:, None, :]
    return (kpos[:, None, :] >= 0) & (kc <= qc) & (kc >= qc - N_PREV_CHUNKS)


def band_attend(q, k, v, rel, mask, rel_bias):
    s = jnp.einsum('ncqhd,nckhd->nchqk', q, k).astype(jnp.float32) * ATTN_SCALE
    bias = rel_bias.astype(jnp.float32)[:, jnp.clip(rel, -REL_CLIP, REL_CLIP) + REL_CLIP]
    s = jnp.where(mask[None, :, None], s + bias[None, None], -jnp.inf)
    p = jax.nn.softmax(s, axis=-1).astype(v.dtype)
    return jnp.einsum('nchqk,nckhd->ncqhd', p, v)


def band_prompt(q, k, v, rel_bias):
    n, s, h, d = q.shape
    nc = s // CHUNK

    def gather_band(x):
        xc = jnp.pad(x, ((0, 0), (BAND_PAST, 0), (0, 0), (0, 0))).reshape(n, nc + N_PREV_CHUNKS, CHUNK, h, d)
        xb = jnp.stack([xc[:, j:j + nc] for j in range(BAND_CHUNKS)], axis=2)
        return xb.reshape(n, nc, BAND_CHUNKS * CHUNK, h, d)

    start = jnp.arange(nc)[:, None] * CHUNK
    qpos = start + jnp.arange(CHUNK)[None, :]
    kpos = start - BAND_PAST + jnp.arange(BAND_CHUNKS * CHUNK)[None, :]
    rel = qpos[0][:, None] - kpos[0][None, :]
    o = band_attend(q.reshape(n, nc, CHUNK, h, d), gather_band(k), gather_band(v),
                    rel, band_mask(qpos, kpos), rel_bias)
    return o.reshape(n, s, h, d)


def band_sample(q, k, v, cache_k, cache_v, p_len, rel_bias):
    lb, t = cache_k.shape[1], q.shape[1]
    qpos = p_len + jnp.arange(t)
    kpos = p_len - lb + jnp.arange(lb + t)
    k_all = jnp.concatenate([cache_k, k], axis=1)[:, None]
    v_all = jnp.concatenate([cache_v, v], axis=1)[:, None]
    o = band_attend(q[:, None], k_all, v_all, qpos[:, None] - kpos[None, :],
                    band_mask(qpos[None], kpos[None]), rel_bias)
    return o[:, 0]


def mixer_prompt(h, w_in, b_f, rel_bias, g_a, g_b, w_out):
    qa, ka, va, logf, qb, kb, vb = project(h, w_in, b_f)
    y = merge(fox_prompt(qa, ka, va, logf), band_prompt(qb, kb, vb, rel_bias), g_a, g_b, w_out)
    lb = min(BAND_PAST, h.shape[1])
    return y, (ka, va, logf, kb[:, -lb:], vb[:, -lb:])


def mixer_sample(h, cache_k_a, cache_v_a, cache_logf_a, cache_k_b, cache_v_b, w_in, b_f, rel_bias, g_a, g_b, w_out):
    qa, ka, va, logf, qb, kb, vb = project(h, w_in, b_f)
    p_len = cache_k_a.shape[1]
    o_a = fox_sample(qa, ka, va, logf, cache_k_a, cache_v_a, cache_logf_a)
    o_b = band_sample(qb, kb, vb, cache_k_b, cache_v_b, p_len, rel_bias)
    return merge(o_a, o_b, g_a, g_b, w_out), (ka, va, logf, kb, vb)


def hier_moe(h, w_rg, b_rg, w_re, b_re, w_gate, w_up, w_down):
    n, l, d = h.shape
    x = h.reshape(n * l, d)
    g_logits = (x @ w_rg + b_rg).astype(jnp.float32)
    g_idx = jnp.argmax(g_logits, axis=-1)
    g_w = jnp.take_along_axis(jax.nn.softmax(g_logits, axis=-1), g_idx[:, None], axis=1)[:, 0]
    e_logits = (x @ w_re + b_re).astype(jnp.float32).reshape(-1, N_GROUPS, EXPERTS_PER_GROUP)
    e_in = jnp.take_along_axis(e_logits, g_idx[:, None, None], axis=1)[:, 0]
    e_val, e_idx = lax.top_k(e_in, TOP_K_IN_GROUP)
    e_w = jax.nn.softmax(e_val, axis=-1) * g_w[:, None]
    expert_id = g_idx[:, None] * EXPERTS_PER_GROUP + e_idx
    combine = jnp.einsum('tke,tk->te', jax.nn.one_hot(expert_id, N_EXPERTS, dtype=jnp.float32), e_w)

    def expert_step(acc, args):
        wg, wu, wd, cw = args
        hh = jax.nn.silu(x @ wg) * (x @ wu)
        return acc + (hh * cw[:, None].astype(hh.dtype)) @ wd, None

    y, _ = lax.scan(expert_step, jnp.zeros_like(x), (w_gate, w_up, w_down, combine.T))
    return y.reshape(n, l, d)


def setup_inputs(seed: int = 0) -> dict:
    key = jax.random.key(seed)
    ks = jax.random.split(key, 32)
    f32 = jnp.float32

    def nrm(k, shape, s=1.0):
        return s * jax.random.normal(k, shape, f32)

    lb = min(BAND_PAST, PAST_LEN)
    return {
        'x_prompt': nrm(ks[0], (BATCH, SEQ, D_MODEL)),
        'x_sample': nrm(ks[1], (DEC_BATCH, DEC_SEQ, D_MODEL)),
        'cache_k_a': nrm(ks[2], (DEPTH, DEC_BATCH, PAST_LEN, H_A, HEAD_DIM)),
        'cache_v_a': nrm(ks[3], (DEPTH, DEC_BATCH, PAST_LEN, H_A, HEAD_DIM)),
        'cache_logf_a': jax.nn.log_sigmoid(FORGET_BIAS_INIT + nrm(ks[4], (DEPTH, DEC_BATCH, PAST_LEN, H_A))),
        'cache_k_b': nrm(ks[5], (DEPTH, DEC_BATCH, lb, H_B, HEAD_DIM)),
        'cache_v_b': nrm(ks[6], (DEPTH, DEC_BATCH, lb, H_B, HEAD_DIM)),
        'c_prompt': nrm(ks[7], (BATCH, D_MODEL)),
        'c_sample': nrm(ks[8], (DEC_BATCH, D_MODEL)),
        'w_ada': nrm(ks[9], (DEPTH, D_MODEL, N_MOD * D_MODEL), 0.5 * D_MODEL ** -0.5),
        'b_ada': nrm(ks[10], (DEPTH, N_MOD * D_MODEL), 0.02),
        'norm_mix': 1.0 + nrm(ks[11], (DEPTH, D_MODEL), 0.1),
        'norm_ffn': 1.0 + nrm(ks[12], (DEPTH, D_MODEL), 0.1),
        'w_in': nrm(ks[13], (DEPTH, D_MODEL, IN_COLS), D_MODEL ** -0.5),
        'b_fgate': FORGET_BIAS_INIT + nrm(ks[14], (DEPTH, H_A), 0.5),
        'rel_bias': nrm(ks[15], (DEPTH, H_B, N_REL), 0.5),
        'onorm_a': 1.0 + nrm(ks[16], (DEPTH, W_A), 0.1),
        'onorm_b': 1.0 + nrm(ks[17], (DEPTH, W_B), 0.1),
        'w_out': nrm(ks[18], (DEPTH, MIX_WIDTH, D_MODEL), MIX_WIDTH ** -0.5),
        'w_router_group': nrm(ks[19], (DEPTH, D_MODEL, N_GROUPS), D_MODEL ** -0.5),
        'b_router_group': nrm(ks[20], (DEPTH, N_GROUPS), 0.01),
        'w_router_expert': nrm(ks[21], (DEPTH, D_MODEL, N_EXPERTS), D_MODEL ** -0.5),
        'b_router_expert': nrm(ks[22], (DEPTH, N_EXPERTS), 0.01),
        'w_gate': nrm(ks[23], (DEPTH, N_EXPERTS, D_MODEL, D_EXPERT), D_MODEL ** -0.5),
        'w_up': nrm(ks[24], (DEPTH, N_EXPERTS, D_MODEL, D_EXPERT), D_MODEL ** -0.5),
        'w_down': nrm(ks[25], (DEPTH, N_EXPERTS, D_EXPERT, D_MODEL), D_EXPERT ** -0.5),
        'norm_final': 1.0 + nrm(ks[26], (D_MODEL,), 0.1),
    }


def reference(x_prompt, x_sample, cache_k_a, cache_v_a, cache_logf_a, cache_k_b, cache_v_b,
              c_prompt, c_sample, w_ada, b_ada, norm_mix, norm_ffn, w_in, b_fgate, rel_bias,
              onorm_a, onorm_b, w_out, w_router_group, b_router_group, w_router_expert,
              b_router_expert, w_gate, w_up, w_down, norm_final):
    xp, xs = x_prompt, x_sample
    st_p, st_s = [], []
    for l in range(DEPTH):
        mix_w = (w_in[l], b_fgate[l], rel_bias[l], onorm_a[l], onorm_b[l], w_out[l])
        moe_w = (w_router_group[l], b_router_group[l], w_router_expert[l], b_router_expert[l],
                 w_gate[l], w_up[l], w_down[l])
        sh1, sc1, g1, sh2, sc2, g2 = modulation(c_prompt, w_ada[l], b_ada[l])
        y, st = mixer_prompt(modulate(xp, norm_mix[l], sh1, sc1), *mix_w)
        xp = xp + g1 * y
        xp = xp + g2 * hier_moe(modulate(xp, norm_ffn[l], sh2, sc2), *moe_w)
        st_p.append(st)
        sh1, sc1, g1, sh2, sc2, g2 = modulation(c_sample, w_ada[l], b_ada[l])
        y, st = mixer_sample(modulate(xs, norm_mix[l], sh1, sc1), cache_k_a[l], cache_v_a[l],
                             cache_logf_a[l], cache_k_b[l], cache_v_b[l], *mix_w)
        xs = xs + g1 * y
        xs = xs + g2 * hier_moe(modulate(xs, norm_ffn[l], sh2, sc2), *moe_w)
        st_s.append(st)

    def stack(sts, i):
        return jnp.stack([st[i] for st in sts], axis=0)

    y_prompt = rmsnorm(xp, norm_final)
    y_sample = rmsnorm(xs, norm_final)
    return (y_prompt, y_sample,
            stack(st_p, 0), stack(st_p, 1), stack(st_p, 2), stack(st_p, 3), stack(st_p, 4),
            stack(st_s, 0), stack(st_s, 1), stack(st_s, 2), stack(st_s, 3), stack(st_s, 4))
```

```python
import functools

import jax
import jax.numpy as jnp
from jax import lax
from jax.experimental import pallas as pl
from jax.experimental.pallas import tpu as pltpu

F32 = jnp.float32
BF16 = jnp.bfloat16

D_MODEL = 1024
HEAD_DIM = 64
N_HEADS = 8
GROUP_W = N_HEADS * HEAD_DIM
CHUNK = 64
N_PREV_CHUNKS = 8
REL_CLIP = 128
N_GROUPS = 4
EXPERTS_PER_GROUP = 8
N_EXPERTS = N_GROUPS * EXPERTS_PER_GROUP
D_EXPERT = D_MODEL // 4
N_MOD = 6
RMS_EPS = 1e-6
ATTN_SCALE = HEAD_DIM ** -0.5
LOG2E = 1.4426950408889634
Q_SCALE = ATTN_SCALE * LOG2E
NEG = -1e30

LANES = 128
N_PAIRS = GROUP_W // LANES
TM = 512
MOD_ROWS = 64
MODS_PER_TILE = TM // MOD_ROWS
FGATE_PAD = LANES
IN_COLS_PAD = 6 * GROUP_W + FGATE_PAD
ROUTER_PAD = LANES
V7X_VMEM_LIMIT = 56 * 1024 * 1024


def _cparams(n_axes):
    return pltpu.CompilerParams(
        dimension_semantics=("arbitrary",) * n_axes,
        vmem_limit_bytes=V7X_VMEM_LIMIT)


def _dot(a, b):
    return jnp.dot(a, b, preferred_element_type=F32)


def _dot_nt(a, b):
    return lax.dot_general(a, b, (((1,), (1,)), ((), ())), preferred_element_type=F32)


def _silu(x):
    return x * jax.nn.sigmoid(x)


def _log_sigmoid(x):
    return jnp.minimum(x, 0.0) - jnp.log1p(jnp.exp(-jnp.abs(x)))


def _ada_kernel(c_ref, w_ref, b_ref, o_ref):
    s = _silu(c_ref[...]).astype(BF16)
    o_ref[0] = _dot(s, w_ref[0].astype(BF16)) + b_ref[0]


def _ada_modulation(c_all, w_ada, b_ada):
    depth = w_ada.shape[0]
    rows = c_all.shape[0]
    n_col = w_ada.shape[2] // D_MODEL
    return pl.pallas_call(
        _ada_kernel,
        grid=(depth, n_col),
        in_specs=[
            pl.BlockSpec((rows, D_MODEL), lambda l, j: (0, 0)),
            pl.BlockSpec((1, D_MODEL, D_MODEL), lambda l, j: (l, 0, j)),
            pl.BlockSpec((1, 1, D_MODEL), lambda l, j: (l, 0, j)),
        ],
        out_specs=pl.BlockSpec((1, rows, D_MODEL), lambda l, j: (l, 0, j)),
        out_shape=jax.ShapeDtypeStruct((depth, rows, w_ada.shape[2]), F32),
        compiler_params=_cparams(2),
        name="ada_modulation",
    )(c_all, w_ada, b_ada.reshape(depth, 1, -1))


def _modulated_norm(x, gain, scale, shift):
    ms = jnp.mean(x * x, axis=-1, keepdims=True)
    y = x * lax.rsqrt(ms + RMS_EPS) * gain
    return y * (1.0 + scale) + shift


def _inproj_kernel(x_ref, sh_ref, sc_ref, g_ref, w_ref, bf_ref,
                   qa_ref, ka_ref, va_ref, qb_ref, kb_ref, vb_ref,
                   kaf_ref, vaf_ref, kbf_ref, vbf_ref, logf_ref, h_scr):
    gain = g_ref[...]
    for r in range(MODS_PER_TILE):
        rows = pl.ds(r * MOD_ROWS, MOD_ROWS)
        h = _modulated_norm(x_ref[rows, :], gain, sc_ref[r:r + 1, :], sh_ref[r:r + 1, :])
        h_scr[rows, :] = h.astype(BF16)
    h = h_scr[...]

    def proj(block):
        return _dot(h, w_ref[:, block * GROUP_W:(block + 1) * GROUP_W])

    qa_ref[...] = (proj(0) * Q_SCALE).astype(BF16)
    ka = proj(1)
    kaf_ref[...] = ka
    ka_ref[...] = ka.astype(BF16)
    va = proj(2)
    vaf_ref[...] = va
    va_ref[...] = va.astype(BF16)
    qb_ref[...] = (proj(3) * Q_SCALE).astype(BF16)
    kb = proj(4)
    kbf_ref[...] = kb
    kb_ref[...] = kb.astype(BF16)
    vb = proj(5)
    vbf_ref[...] = vb
    vb_ref[...] = vb.astype(BF16)
    fa = _dot(h, w_ref[:, 6 * GROUP_W:]) + bf_ref[...]
    logf_ref[...] = _log_sigmoid(fa)


def _in_projection(x, shift_g, scale_g, gain, w_bf16, bf_pad):
    t = x.shape[0]
    n_tiles = t // TM
    row_spec = lambda w: pl.BlockSpec((TM, w), lambda i: (i, 0))
    mod_spec = pl.BlockSpec((MODS_PER_TILE, D_MODEL), lambda i: (i, 0))
    const = lambda shape: pl.BlockSpec(shape, lambda i: (0, 0))
    bf_out = jax.ShapeDtypeStruct((t, GROUP_W), BF16)
    f_out = jax.ShapeDtypeStruct((t, GROUP_W), F32)
    return pl.pallas_call(
        _inproj_kernel,
        grid=(n_tiles,),
        in_specs=[row_spec(D_MODEL), mod_spec, mod_spec, const((1, D_MODEL)),
                  const((D_MODEL, IN_COLS_PAD)), const((1, FGATE_PAD))],
        out_specs=[row_spec(GROUP_W)] * 10 + [row_spec(FGATE_PAD)],
        out_shape=[bf_out] * 6 + [f_out] * 4 + [jax.ShapeDtypeStruct((t, FGATE_PAD), F32)],
        scratch_shapes=[pltpu.VMEM((TM, D_MODEL), BF16)],
        compiler_params=_cparams(1),
        name="in_projection",
    )(x, shift_g, scale_g, gain, w_bf16, bf_pad)


def _cumsum_kernel(x_ref, o_ref):
    x = x_ref[...]
    length = x.shape[1]
    lane = lax.broadcasted_iota(jnp.int32, x.shape, 1)
    step = 1
    while step < length:
        x = x + jnp.where(lane >= step, pltpu.roll(x, step, 1), 0.0)
        step *= 2
    o_ref[...] = x * LOG2E


def _cumsum_lanes(x, block_rows, block_len):
    rows, length = x.shape
    return pl.pallas_call(
        _cumsum_kernel,
        grid=(rows // block_rows, length // block_len),
        in_specs=[pl.BlockSpec((block_rows, block_len), lambda i, j: (i, j))],
        out_specs=pl.BlockSpec((block_rows, block_len), lambda i, j: (i, j)),
        out_shape=jax.ShapeDtypeStruct((rows, length), F32),
        compiler_params=_cparams(2),
        name="cumsum_lanes",
    )(x)


def _split_pair(q):
    lane = lax.broadcasted_iota(jnp.int32, q.shape, 1)
    low = lane < HEAD_DIM
    zero = jnp.zeros_like(q)
    return (jnp.where(low, q, zero), jnp.where(low, zero, q)), low


def _pair_norm_merge(o_pair, low, gain):
    normed = []
    for h, o in enumerate(o_pair):
        own = low if h == 0 else jnp.logical_not(low)
        ms = jnp.sum(jnp.where(own, o * o, 0.0), axis=1, keepdims=True) * (1.0 / HEAD_DIM)
        normed.append(o * lax.rsqrt(ms + RMS_EPS))
    return jnp.where(low, normed[0], normed[1]) * gain


def _fox_kernel(q_ref, k_ref, v_ref, fc_ref, fr_ref, g_ref, o_ref, m_scr, l_scr, acc_scr):
    qi = pl.program_id(2)
    qh, low = _split_pair(q_ref[...])
    fq = (fc_ref[0, :, 0:1], fc_ref[0, :, 1:2])
    m_scr[...] = jnp.full(m_scr.shape, NEG, F32)
    l_scr[...] = jnp.zeros(l_scr.shape, F32)
    acc_scr[...] = jnp.zeros(acc_scr.shape, F32)
    row = lax.broadcasted_iota(jnp.int32, (TM, TM), 0)
    col = lax.broadcasted_iota(jnp.int32, (TM, TM), 1)
    causal = col <= row

    def tile(start, frow, masked):
        k = k_ref[pl.ds(start, TM), :]
        v = v_ref[pl.ds(start, TM), :]
        for h in range(2):
            t = _dot_nt(qh[h], k) - frow[h:h + 1, :]
            if masked:
                t = jnp.where(causal, t, NEG)
            m_old = m_scr[h]
            m_new = jnp.maximum(m_old, jnp.max(t, axis=1, keepdims=True) + fq[h])
            p = jnp.exp2(t + (fq[h] - m_new))
            alpha = jnp.exp2(m_old - m_new)
            l_scr[h] = alpha * l_scr[h] + jnp.sum(p, axis=1, keepdims=True)
            acc_scr[h] = alpha * acc_scr[h] + _dot(p.astype(BF16), v)
            m_scr[h] = m_new

    def body(ki, carry):
        tile(pl.multiple_of(ki * TM, TM), fr_ref[0, 0, ki], False)
        return carry

    lax.fori_loop(0, qi, body, 0)
    tile(pl.multiple_of(qi * TM, TM), fr_ref[0, 0, qi], True)
    o_pair = [acc_scr[h] / l_scr[h] for h in range(2)]
    o_ref[...] = _pair_norm_merge(o_pair, low, g_ref[...]).astype(BF16)


def _fox_prompt(qa, ka, va, f_col, f_row, gain, n_batch, seq):
    nq = seq // TM
    return pl.pallas_call(
        _fox_kernel,
        grid=(n_batch, N_PAIRS, nq),
        in_specs=[
            pl.BlockSpec((TM, LANES), lambda n, p, i: (n * nq + i, p)),
            pl.BlockSpec((seq, LANES), lambda n, p, i: (n, p)),
            pl.BlockSpec((seq, LANES), lambda n, p, i: (n, p)),
            pl.BlockSpec((1, TM, 2), lambda n, p, i: (p, n * nq + i, 0)),
            pl.BlockSpec((1, 1, nq, 2, TM), lambda n, p, i: (p, n, 0, 0, 0)),
            pl.BlockSpec((1, LANES), lambda n, p, i: (0, p)),
        ],
        out_specs=pl.BlockSpec((TM, LANES), lambda n, p, i: (n * nq + i, p)),
        out_shape=jax.ShapeDtypeStruct((n_batch * seq, GROUP_W), BF16),
        scratch_shapes=[pltpu.VMEM((2, TM, 1), F32), pltpu.VMEM((2, TM, 1), F32),
                        pltpu.VMEM((2, TM, LANES), F32)],
        compiler_params=_cparams(3),
        name="fox_prompt",
    )(qa, ka, va, f_col, f_row, gain)


def _band_kernel(q_ref, k_ref, v_ref, bm_ref, g_ref, o_ref):
    qi = pl.program_id(2)
    qh, low = _split_pair(q_ref[...])
    prev = pl.multiple_of(jnp.maximum(qi - 1, 0) * TM, TM)
    own = pl.multiple_of(qi * TM, TM)
    k0 = k_ref[pl.ds(prev, TM), :]
    v0 = v_ref[pl.ds(prev, TM), :]
    k1 = k_ref[pl.ds(own, TM), :]
    v1 = v_ref[pl.ds(own, TM), :]
    has_prev = qi > 0
    o_pair = []
    for h in range(2):
        s0 = jnp.where(has_prev, _dot_nt(qh[h], k0) + bm_ref[0, h, :, :TM], NEG)
        s1 = _dot_nt(qh[h], k1) + bm_ref[0, h, :, TM:]
        m = jnp.maximum(jnp.max(s0, axis=1, keepdims=True), jnp.max(s1, axis=1, keepdims=True))
        p0 = jnp.exp2(s0 - m)
        p1 = jnp.exp2(s1 - m)
        l = jnp.sum(p0, axis=1, keepdims=True) + jnp.sum(p1, axis=1, keepdims=True)
        o_pair.append((_dot(p0.astype(BF16), v0) + _dot(p1.astype(BF16), v1)) / l)
    o_ref[...] = _pair_norm_merge(o_pair, low, g_ref[...]).astype(BF16)


def _band_prompt(qb, kb, vb, bias_mask, gain, n_batch, seq):
    nq = seq // TM
    return pl.pallas_call(
        _band_kernel,
        grid=(N_PAIRS, n_batch, nq),
        in_specs=[
            pl.BlockSpec((TM, LANES), lambda p, n, i: (n * nq + i, p)),
            pl.BlockSpec((seq, LANES), lambda p, n, i: (n, p)),
            pl.BlockSpec((seq, LANES), lambda p, n, i: (n, p)),
            pl.BlockSpec((1, 2, TM, 2 * TM), lambda p, n, i: (p, 0, 0, 0)),
            pl.BlockSpec((1, LANES), lambda p, n, i: (0, p)),
        ],
        out_specs=pl.BlockSpec((TM, LANES), lambda p, n, i: (n * nq + i, p)),
        out_shape=jax.ShapeDtypeStruct((n_batch * seq, GROUP_W), BF16),
        compiler_params=_cparams(3),
        name="band_prompt",
    )(qb, kb, vb, bias_mask, gain)


def _two_part_attention(z_cache, z_new, v_cache, v_new):
    m = jnp.maximum(jnp.max(z_cache, axis=1, keepdims=True), jnp.max(z_new, axis=1, keepdims=True))
    p_c = jnp.exp2(z_cache - m)
    p_n = jnp.exp2(z_new - m)
    l = jnp.sum(p_c, axis=1, keepdims=True) + jnp.sum(p_n, axis=1, keepdims=True)
    return (_dot(p_c.astype(BF16), v_cache) + _dot(p_n.astype(BF16), v_new)) / l


def _sample_attn_kernel(qa_ref, ka_ref, va_ref, qb_ref, kb_ref, vb_ref,
                        cka_ref, cva_ref, ckb_ref, cvb_ref, fq_ref, fk_ref,
                        bc_ref, bn_ref, ga_ref, gb_ref, oa_ref, ob_ref):
    t_new = qa_ref.shape[0]
    past_a = cka_ref.shape[1]
    row = lax.broadcasted_iota(jnp.int32, (t_new, t_new), 0)
    col = lax.broadcasted_iota(jnp.int32, (t_new, t_new), 1)
    causal = col <= row
    for p in range(N_PAIRS):
        lanes = slice(p * LANES, (p + 1) * LANES)
        qh, low = _split_pair(qa_ref[:, lanes])
        k_c = cka_ref[0, :, lanes].astype(BF16)
        v_c = cva_ref[0, :, lanes].astype(BF16)
        k_n = ka_ref[:, lanes]
        v_n = va_ref[:, lanes]
        o_pair = []
        for j in range(2):
            h = 2 * p + j
            fq = fq_ref[0, :, h:h + 1]
            z_c = _dot_nt(qh[j], k_c) + fq - fk_ref[0, h:h + 1, :past_a]
            z_n = _dot_nt(qh[j], k_n) + fq - fk_ref[0, h:h + 1, past_a:past_a + t_new]
            z_n = jnp.where(causal, z_n, NEG)
            o_pair.append(_two_part_attention(z_c, z_n, v_c, v_n))
        oa_ref[:, lanes] = _pair_norm_merge(o_pair, low, ga_ref[:, lanes]).astype(BF16)
        qh, low = _split_pair(qb_ref[:, lanes])
        k_c = ckb_ref[0, :, lanes].astype(BF16)
        v_c = cvb_ref[0, :, lanes].astype(BF16)
        k_n = kb_ref[:, lanes]
        v_n = vb_ref[:, lanes]
        o_pair = []
        for j in range(2):
            h = 2 * p + j
            z_c = _dot_nt(qh[j], k_c) + bc_ref[h]
            z_n = _dot_nt(qh[j], k_n) + bn_ref[h]
            o_pair.append(_two_part_attention(z_c, z_n, v_c, v_n))
        ob_ref[:, lanes] = _pair_norm_merge(o_pair, low, gb_ref[:, lanes]).astype(BF16)


def _sample_attention(proj, first_row_block, cka, cva, ckb, cvb, fq, fk, bias_c, bias_n, ga, gb):
    qa, ka, va, qb, kb, vb = proj
    n_b, past_a = cka.shape[0], cka.shape[1]
    past_b = ckb.shape[1]
    t_new = fq.shape[1]
    new_spec = pl.BlockSpec((t_new, GROUP_W), lambda b: (first_row_block + b, 0))
    full = lambda a: pl.BlockSpec(a.shape, lambda b: (0,) * a.ndim)
    out = jax.ShapeDtypeStruct((n_b * t_new, GROUP_W), BF16)
    return pl.pallas_call(
        _sample_attn_kernel,
        grid=(n_b,),
        in_specs=[new_spec] * 6 + [
            pl.BlockSpec((1, past_a, GROUP_W), lambda b: (b, 0, 0)),
            pl.BlockSpec((1, past_a, GROUP_W), lambda b: (b, 0, 0)),
            pl.BlockSpec((1, past_b, GROUP_W), lambda b: (b, 0, 0)),
            pl.BlockSpec((1, past_b, GROUP_W), lambda b: (b, 0, 0)),
            pl.BlockSpec((1, t_new, N_HEADS), lambda b: (b, 0, 0)),
            pl.BlockSpec((1, N_HEADS, fk.shape[2]), lambda b: (b, 0, 0)),
            full(bias_c), full(bias_n), full(ga), full(gb)],
        out_specs=[pl.BlockSpec((t_new, GROUP_W), lambda b: (b, 0))] * 2,
        out_shape=[out, out],
        compiler_params=_cparams(1),
        name="sample_attention",
    )(qa, ka, va, qb, kb, vb, cka, cva, ckb, cvb, fq, fk, bias_c, bias_n, ga, gb)


def _route(logits):
    lane = lax.broadcasted_iota(jnp.int32, logits.shape, 1).astype(F32)
    far = float(4 * LANES)
    is_group = (lane >= N_EXPERTS) & (lane < N_EXPERTS + N_GROUPS)
    gl = jnp.where(is_group, logits, NEG)
    g_max = jnp.max(gl, axis=1, keepdims=True)
    g_idx = jnp.min(jnp.where(gl == g_max, lane, far), axis=1, keepdims=True) - N_EXPERTS
    g_sum = jnp.sum(jnp.where(is_group, jnp.exp(gl - g_max), 0.0), axis=1, keepdims=True)
    g_w = 1.0 / g_sum
    first = g_idx * EXPERTS_PER_GROUP
    in_group = (lane >= first) & (lane < first + EXPERTS_PER_GROUP)
    el = jnp.where(in_group, logits, NEG)
    v1 = jnp.max(el, axis=1, keepdims=True)
    i1 = jnp.min(jnp.where(el == v1, lane, far), axis=1, keepdims=True)
    el2 = jnp.where(lane == i1, NEG, el)
    v2 = jnp.max(el2, axis=1, keepdims=True)
    i2 = jnp.min(jnp.where(el2 == v2, lane, far), axis=1, keepdims=True)
    e2 = jnp.exp(v2 - v1)
    w1 = g_w / (1.0 + e2)
    w2 = g_w * e2 / (1.0 + e2)
    return jnp.where(lane == i1, w1, 0.0) + jnp.where(lane == i2, w2, 0.0)


def _merge_kernel(oa_ref, ob_ref, x_ref, g1_ref, sh_ref, sc_ref, wo_ref, nf_ref,
                  wrh_ref, wrl_ref, br_ref, x1_ref, h2_ref, cmb_ref, hf_scr):
    y = _dot(oa_ref[...], wo_ref[:GROUP_W, :]) + _dot(ob_ref[...], wo_ref[GROUP_W:, :])
    gain = nf_ref[...]
    for r in range(MODS_PER_TILE):
        rows = pl.ds(r * MOD_ROWS, MOD_ROWS)
        x1 = x_ref[rows, :] + g1_ref[r:r + 1, :] * y[r * MOD_ROWS:(r + 1) * MOD_ROWS, :]
        x1_ref[rows, :] = x1
        h2 = _modulated_norm(x1, gain, sc_ref[r:r + 1, :], sh_ref[r:r + 1, :])
        hf_scr[rows, :] = h2
        h2_ref[rows, :] = h2.astype(BF16)
    h = hf_scr[...]
    h_hi = h.astype(BF16)
    h_lo = (h - h_hi.astype(F32)).astype(BF16)
    logits = (_dot(h_hi, wrh_ref[...]) + _dot(h_lo, wrh_ref[...])
              + _dot(h_hi, wrl_ref[...]) + br_ref[...])
    cmb_ref[...] = _route(logits)


def _merge_and_route(oa, ob, x, gate_g, shift_g, scale_g, wo, norm_ffn, wr_hi, wr_lo, br):
    t = x.shape[0]
    row_spec = lambda w: pl.BlockSpec((TM, w), lambda i: (i, 0))
    mod_spec = pl.BlockSpec((MODS_PER_TILE, D_MODEL), lambda i: (i, 0))
    const = lambda shape: pl.BlockSpec(shape, lambda i: (0, 0))
    return pl.pallas_call(
        _merge_kernel,
        grid=(t // TM,),
        in_specs=[row_spec(GROUP_W), row_spec(GROUP_W), row_spec(D_MODEL),
                  mod_spec, mod_spec, mod_spec, const((D_MODEL, D_MODEL)), const((1, D_MODEL)),
                  const((D_MODEL, ROUTER_PAD)), const((D_MODEL, ROUTER_PAD)), const((1, ROUTER_PAD))],
        out_specs=[row_spec(D_MODEL), row_spec(D_MODEL), row_spec(ROUTER_PAD)],
        out_shape=[jax.ShapeDtypeStruct((t, D_MODEL), F32),
                   jax.ShapeDtypeStruct((t, D_MODEL), BF16),
                   jax.ShapeDtypeStruct((t, ROUTER_PAD), F32)],
        scratch_shapes=[pltpu.VMEM((TM, D_MODEL), F32)],
        compiler_params=_cparams(1),
        name="merge_and_route",
    )(oa, ob, x, gate_g, shift_g, scale_g, wo, norm_ffn, wr_hi, wr_lo, br)


def _moe_kernel(h_ref, wg_ref, wu_ref, wd_ref, cmb_ref, x1_ref, g2_ref, o_ref, acc_scr):
    e = pl.program_id(1)

    @pl.when(e == 0)
    def _():
        acc_scr[...] = jnp.zeros(acc_scr.shape, F32)

    h = h_ref[...]
    lane = lax.broadcasted_iota(jnp.int32, cmb_ref.shape, 1)
    cw = jnp.sum(jnp.where(lane == e, cmb_ref[...], 0.0), axis=1, keepdims=True)
    hh = _silu(_dot(h, wg_ref[0])) * _dot(h, wu_ref[0]) * cw
    acc_scr[...] += _dot(hh.astype(BF16), wd_ref[0])

    @pl.when(e == N_EXPERTS - 1)
    def _():
        for r in range(MODS_PER_TILE):
            rows = pl.ds(r * MOD_ROWS, MOD_ROWS)
            o_ref[rows, :] = x1_ref[rows, :] + g2_ref[r:r + 1, :] * acc_scr[rows, :]


def _moe(h2, wg, wu, wd, cmb, x1, gate_g):
    t = h2.shape[0]
    return pl.pallas_call(
        _moe_kernel,
        grid=(t // TM, N_EXPERTS),
        in_specs=[
            pl.BlockSpec((TM, D_MODEL), lambda i, e: (i, 0)),
            pl.BlockSpec((1, D_MODEL, D_EXPERT), lambda i, e: (e, 0, 0)),
            pl.BlockSpec((1, D_MODEL, D_EXPERT), lambda i, e: (e, 0, 0)),
            pl.BlockSpec((1, D_EXPERT, D_MODEL), lambda i, e: (e, 0, 0)),
            pl.BlockSpec((TM, ROUTER_PAD), lambda i, e: (i, 0)),
            pl.BlockSpec((TM, D_MODEL), lambda i, e: (i, 0)),
            pl.BlockSpec((MODS_PER_TILE, D_MODEL), lambda i, e: (i, 0)),
        ],
        out_specs=pl.BlockSpec((TM, D_MODEL), lambda i, e: (i, 0)),
        out_shape=jax.ShapeDtypeStruct((t, D_MODEL), F32),
        scratch_shapes=[pltpu.VMEM((TM, D_MODEL), F32)],
        compiler_params=_cparams(2),
        name="moe_experts",
    )(h2, wg, wu, wd, cmb, x1, gate_g)


def _final_norm_kernel(x_ref, g_ref, o_ref):
    x = x_ref[...]
    ms = jnp.mean(x * x, axis=-1, keepdims=True)
    o_ref[...] = x * lax.rsqrt(ms + RMS_EPS) * g_ref[...]


def _final_norm(x, gain):
    t = x.shape[0]
    return pl.pallas_call(
        _final_norm_kernel,
        grid=(t // TM,),
        in_specs=[pl.BlockSpec((TM, D_MODEL), lambda i: (i, 0)),
                  pl.BlockSpec((1, D_MODEL), lambda i: (0, 0))],
        out_specs=pl.BlockSpec((TM, D_MODEL), lambda i: (i, 0)),
        out_shape=jax.ShapeDtypeStruct((t, D_MODEL), F32),
        compiler_params=_cparams(1),
        name="final_norm",
    )(x, gain)


def _rel_bias_slab(rel_bias, rel):
    return rel_bias[:, jnp.clip(rel, -REL_CLIP, REL_CLIP) + REL_CLIP] * LOG2E


def _band_prompt_bias(rel_bias):
    i = jnp.arange(TM)[:, None]
    j = jnp.arange(2 * TM)[None, :]
    bias = _rel_bias_slab(rel_bias, TM + i - j)
    ic, jc = i // CHUNK, j // CHUNK
    visible = (jc >= ic) & (jc <= ic + N_PREV_CHUNKS)
    return jnp.where(visible[None], bias, NEG).reshape(N_PAIRS, 2, TM, 2 * TM)


def _pack_w_in(w_in, b_fgate):
    splits = (GROUP_W, 2 * GROUP_W, 3 * GROUP_W, 3 * GROUP_W + N_HEADS,
              4 * GROUP_W + N_HEADS, 5 * GROUP_W + N_HEADS)
    qa, ka, va, fa, qb, kb, vb = jnp.split(w_in, splits, axis=-1)
    fa = jnp.pad(fa, ((0, 0), (0, FGATE_PAD - N_HEADS)))
    w = jnp.concatenate([qa, ka, va, qb, kb, vb, fa], axis=-1).astype(BF16)
    bf = jnp.pad(b_fgate, (0, FGATE_PAD - N_HEADS)).reshape(1, FGATE_PAD)
    return w, bf


def _pack_router(w_rg, b_rg, w_re, b_re):
    pad = ROUTER_PAD - N_EXPERTS - N_GROUPS
    w = jnp.pad(jnp.concatenate([w_re, w_rg], axis=-1), ((0, 0), (0, pad)))
    b = jnp.pad(jnp.concatenate([b_re, b_rg]), (0, pad)).reshape(1, ROUTER_PAD)
    w_hi = w.astype(BF16)
    w_lo = (w - w_hi.astype(F32)).astype(BF16)
    return w_hi, w_lo, b


def kernel(x_prompt, x_sample, cache_k_a, cache_v_a, cache_logf_a, cache_k_b, cache_v_b, c_prompt, c_sample, w_ada, b_ada, norm_mix, norm_ffn, w_in, b_fgate, rel_bias, onorm_a, onorm_b, w_out, w_router_group, b_router_group, w_router_expert, b_router_expert, w_gate, w_up, w_down, norm_final):
    n_p, seq, _ = x_prompt.shape
    n_s, t_new, _ = x_sample.shape
    depth = w_in.shape[0]
    past_a = cache_k_a.shape[2]
    past_b = cache_k_b.shape[2]
    t_p = n_p * seq
    t_s = n_s * t_new
    nq = seq // TM
    assert seq % TM == 0 and t_s % TM == 0 and TM % t_new == 0 and t_new == MOD_ROWS == CHUNK
    assert past_b == N_PREV_CHUNKS * CHUNK and seq >= past_b

    x = jnp.concatenate([x_prompt.reshape(t_p, D_MODEL), x_sample.reshape(t_s, D_MODEL)], axis=0)

    n_cond = n_p + n_s
    cond_rows = -(-n_cond // 8) * 8
    c_all = jnp.pad(jnp.concatenate([c_prompt, c_sample], axis=0), ((0, cond_rows - n_cond), (0, 0)))
    mod = _ada_modulation(c_all, w_ada, b_ada)
    group_cond = jnp.concatenate([jnp.repeat(jnp.arange(n_p), seq // MOD_ROWS),
                                  n_p + jnp.arange(n_s)])

    pad_len = -(-(past_a + t_new) // LANES) * LANES
    st_p, st_s = [], []
    for l in range(depth):
        sh1, sc1, g1, sh2, sc2, g2 = [m[group_cond] for m in jnp.split(mod[l], N_MOD, axis=-1)]
        w_pack, bf_pad = _pack_w_in(w_in[l], b_fgate[l])
        (qa, ka, va, qb, kb, vb, kaf, vaf, kbf, vbf, logf_pad) = _in_projection(
            x, sh1, sc1, norm_mix[l].reshape(1, D_MODEL), w_pack, bf_pad)
        logf = logf_pad[:, :N_HEADS]

        lf_p = logf[:t_p].reshape(n_p, seq, N_HEADS).transpose(0, 2, 1).reshape(n_p * N_HEADS, seq)
        f_p = _cumsum_lanes(lf_p, N_HEADS, seq).reshape(n_p, N_PAIRS, 2, nq, TM)
        f_row = f_p.transpose(1, 0, 3, 2, 4)
        f_col = f_p.transpose(1, 0, 3, 4, 2).reshape(N_PAIRS, t_p, 2)
        lf_s = jnp.concatenate([cache_logf_a[l].astype(F32), logf[t_p:].reshape(n_s, t_new, N_HEADS)], axis=1)
        lf_s = jnp.pad(lf_s.transpose(0, 2, 1), ((0, 0), (0, 0), (0, pad_len - past_a - t_new)))
        f_s = _cumsum_lanes(lf_s.reshape(n_s * N_HEADS, pad_len), n_s * N_HEADS, pad_len)
        f_s = f_s.reshape(n_s, N_HEADS, pad_len)
        fq_s = f_s[:, :, past_a:past_a + t_new].transpose(0, 2, 1)

        ga = onorm_a[l].reshape(1, GROUP_W)
        gb = onorm_b[l].reshape(1, GROUP_W)
        oa_p = _fox_prompt(qa, ka, va, f_col, f_row, ga, n_p, seq)
        ob_p = _band_prompt(qb, kb, vb, _band_prompt_bias(rel_bias[l]), gb, n_p, seq)

        t_idx = jnp.arange(t_new)[:, None]
        bias_c = _rel_bias_slab(rel_bias[l], past_b + t_idx - jnp.arange(past_b)[None, :])
        bias_n = _rel_bias_slab(rel_bias[l], t_idx - jnp.arange(t_new)[None, :])
        oa_s, ob_s = _sample_attention(
            (qa, ka, va, qb, kb, vb), t_p // t_new,
            cache_k_a[l].reshape(n_s, past_a, GROUP_W), cache_v_a[l].reshape(n_s, past_a, GROUP_W),
            cache_k_b[l].reshape(n_s, past_b, GROUP_W), cache_v_b[l].reshape(n_s, past_b, GROUP_W),
            fq_s, f_s, bias_c, bias_n, ga, gb)
        oa = jnp.concatenate([oa_p, oa_s], axis=0)
        ob = jnp.concatenate([ob_p, ob_s], axis=0)

        wr_hi, wr_lo, br = _pack_router(w_router_group[l], b_router_group[l],
                                        w_router_expert[l], b_router_expert[l])
        x1, h2, cmb = _merge_and_route(oa, ob, x, g1, sh2, sc2, w_out[l].astype(BF16),
                                       norm_ffn[l].reshape(1, D_MODEL), wr_hi, wr_lo, br)
        x = _moe(h2, w_gate[l].astype(BF16), w_up[l].astype(BF16), w_down[l].astype(BF16), cmb, x1, g2)

        heads = lambda a, n, rows: a.reshape(n, rows, N_HEADS, HEAD_DIM)
        st_p.append((heads(kaf[:t_p], n_p, seq), heads(vaf[:t_p], n_p, seq),
                     logf[:t_p].reshape(n_p, seq, N_HEADS),
                     heads(kbf[:t_p], n_p, seq)[:, seq - past_b:], heads(vbf[:t_p], n_p, seq)[:, seq - past_b:]))
        st_s.append((heads(kaf[t_p:], n_s, t_new), heads(vaf[t_p:], n_s, t_new),
                     logf[t_p:].reshape(n_s, t_new, N_HEADS),
                     heads(kbf[t_p:], n_s, t_new), heads(vbf[t_p:], n_s, t_new)))

    y = _final_norm(x, norm_final.reshape(1, D_MODEL))
    stack = lambda sts, i: jnp.stack([st[i] for st in sts], axis=0)
    return (y[:t_p].reshape(n_p, seq, D_MODEL), y[t_p:].reshape(n_s, t_new, D_MODEL),
            stack(st_p, 0), stack(st_p, 1), stack(st_p, 2), stack(st_p, 3), stack(st_p, 4),
            stack(st_s, 0), stack(st_s, 1), stack(st_s, 2), stack(st_s, 3), stack(st_s, 4))
```

```python
import functools

import jax
import jax.numpy as jnp
from jax import lax
from jax.experimental import pallas as pl
from jax.experimental.pallas import tpu as pltpu

F32 = jnp.float32
BF16 = jnp.bfloat16

D_MODEL = 1024
HEAD_DIM = 64
N_HEADS = 8
GROUP_W = N_HEADS * HEAD_DIM
CHUNK = 64
N_PREV_CHUNKS = 8
REL_CLIP = 128
N_GROUPS = 4
EXPERTS_PER_GROUP = 8
N_EXPERTS = N_GROUPS * EXPERTS_PER_GROUP
D_EXPERT = D_MODEL // 4
N_MOD = 6
RMS_EPS = 1e-6
ATTN_SCALE = HEAD_DIM ** -0.5
LOG2E = 1.4426950408889634
Q_SCALE = ATTN_SCALE * LOG2E
NEG = -1e30

LANES = 128
N_PAIRS = GROUP_W // LANES
TM = 512
MOD_ROWS = 64
MODS_PER_TILE = TM // MOD_ROWS
FGATE_PAD = LANES
IN_COLS_PAD = 6 * GROUP_W + FGATE_PAD
ROUTER_PAD = LANES
V7X_VMEM_LIMIT = 56 * 1024 * 1024


def _cparams(n_axes):
    return pltpu.CompilerParams(
        dimension_semantics=("arbitrary",) * n_axes,
        vmem_limit_bytes=V7X_VMEM_LIMIT)


def _dot(a, b):
    return jnp.dot(a, b, preferred_element_type=F32)


def _dot_nt(a, b):
    return lax.dot_general(a, b, (((1,), (1,)), ((), ())), preferred_element_type=F32)


def _silu(x):
    return x * jax.nn.sigmoid(x)


def _log_sigmoid(x):
    return jnp.minimum(x, 0.0) - jnp.log1p(jnp.exp(-jnp.abs(x)))


def _ada_kernel(c_ref, w_ref, b_ref, o_ref):
    s = _silu(c_ref[...]).astype(BF16)
    o_ref[0] = _dot(s, w_ref[0].astype(BF16)) + b_ref[0]


def _ada_modulation(c_all, w_ada, b_ada):
    depth = w_ada.shape[0]
    rows = c_all.shape[0]
    n_col = w_ada.shape[2] // D_MODEL
    return pl.pallas_call(
        _ada_kernel,
        grid=(depth, n_col),
        in_specs=[
            pl.BlockSpec((rows, D_MODEL), lambda l, j: (0, 0)),
            pl.BlockSpec((1, D_MODEL, D_MODEL), lambda l, j: (l, 0, j)),
            pl.BlockSpec((1, 1, D_MODEL), lambda l, j: (l, 0, j)),
        ],
        out_specs=pl.BlockSpec((1, rows, D_MODEL), lambda l, j: (l, 0, j)),
        out_shape=jax.ShapeDtypeStruct((depth, rows, w_ada.shape[2]), F32),
        compiler_params=_cparams(2),
        name="ada_modulation",
    )(c_all, w_ada, b_ada.reshape(depth, 1, -1))


def _modulated_norm(x, gain, scale, shift):
    ms = jnp.mean(x * x, axis=-1, keepdims=True)
    y = x * lax.rsqrt(ms + RMS_EPS) * gain
    return y * (1.0 + scale) + shift


def _inproj_kernel(x_ref, sh_ref, sc_ref, g_ref, w_ref, bf_ref,
                   qa_ref, ka_ref, va_ref, qb_ref, kb_ref, vb_ref,
                   kaf_ref, vaf_ref, kbf_ref, vbf_ref, logf_ref, h_scr):
    gain = g_ref[...]
    for r in range(MODS_PER_TILE):
        rows = pl.ds(r * MOD_ROWS, MOD_ROWS)
        h = _modulated_norm(x_ref[rows, :], gain, sc_ref[r:r + 1, :], sh_ref[r:r + 1, :])
        h_scr[rows, :] = h.astype(BF16)
    h = h_scr[...]

    def proj(block):
        return _dot(h, w_ref[:, block * GROUP_W:(block + 1) * GROUP_W])

    qa_ref[...] = (proj(0) * Q_SCALE).astype(BF16)
    ka = proj(1)
    kaf_ref[...] = ka
    ka_ref[...] = ka.astype(BF16)
    va = proj(2)
    vaf_ref[...] = va
    va_ref[...] = va.astype(BF16)
    qb_ref[...] = (proj(3) * Q_SCALE).astype(BF16)
    kb = proj(4)
    kbf_ref[...] = kb
    kb_ref[...] = kb.astype(BF16)
    vb = proj(5)
    vbf_ref[...] = vb
    vb_ref[...] = vb.astype(BF16)
    fa = _dot(h, w_ref[:, 6 * GROUP_W:]) + bf_ref[...]
    logf_ref[...] = _log_sigmoid(fa)


def _in_projection(x, shift_g, scale_g, gain, w_bf16, bf_pad):
    t = x.shape[0]
    n_tiles = t // TM
    row_spec = lambda w: pl.BlockSpec((TM, w), lambda i: (i, 0))
    mod_spec = pl.BlockSpec((MODS_PER_TILE, D_MODEL), lambda i: (i, 0))
    const = lambda shape: pl.BlockSpec(shape, lambda i: (0, 0))
    bf_out = jax.ShapeDtypeStruct((t, GROUP_W), BF16)
    f_out = jax.ShapeDtypeStruct((t, GROUP_W), F32)
    return pl.pallas_call(
        _inproj_kernel,
        grid=(n_tiles,),
        in_specs=[row_spec(D_MODEL), mod_spec, mod_spec, const((1, D_MODEL)),
                  const((D_MODEL, IN_COLS_PAD)), const((1, FGATE_PAD))],
        out_specs=[row_spec(GROUP_W)] * 10 + [row_spec(FGATE_PAD)],
        out_shape=[bf_out] * 6 + [f_out] * 4 + [jax.ShapeDtypeStruct((t, FGATE_PAD), F32)],
        scratch_shapes=[pltpu.VMEM((TM, D_MODEL), BF16)],
        compiler_params=_cparams(1),
        name="in_projection",
    )(x, shift_g, scale_g, gain, w_bf16, bf_pad)


def _cumsum_kernel(x_ref, o_ref):
    x = x_ref[...]
    length = x.shape[1]
    lane = lax.broadcasted_iota(jnp.int32, x.shape, 1)
    step = 1
    while step < length:
        x = x + jnp.where(lane >= step, pltpu.roll(x, step, 1), 0.0)
        step *= 2
    o_ref[...] = x * LOG2E


def _cumsum_lanes(x, block_rows, block_len):
    rows, length = x.shape
    return pl.pallas_call(
        _cumsum_kernel,
        grid=(rows // block_rows, length // block_len),
        in_specs=[pl.BlockSpec((block_rows, block_len), lambda i, j: (i, j))],
        out_specs=pl.BlockSpec((block_rows, block_len), lambda i, j: (i, j)),
        out_shape=jax.ShapeDtypeStruct((rows, length), F32),
        compiler_params=_cparams(2),
        name="cumsum_lanes",
    )(x)


def _split_pair(q):
    lane = lax.broadcasted_iota(jnp.int32, q.shape, 1)
    low = lane < HEAD_DIM
    zero = jnp.zeros_like(q)
    return (jnp.where(low, q, zero), jnp.where(low, zero, q)), low


def _pair_norm_merge(o_pair, low, gain):
    normed = []
    for h, o in enumerate(o_pair):
        own = low if h == 0 else jnp.logical_not(low)
        ms = jnp.sum(jnp.where(own, o * o, 0.0), axis=1, keepdims=True) * (1.0 / HEAD_DIM)
        normed.append(o * lax.rsqrt(ms + RMS_EPS))
    return jnp.where(low, normed[0], normed[1]) * gain


def _fox_kernel(q_ref, k_ref, v_ref, fc_ref, fr_ref, g_ref, o_ref,
                qh_scr, fq_scr, m_scr, l_scr, acc_scr):
    qi = pl.program_id(2)
    qh, low = _split_pair(q_ref[...])
    for h in range(2):
        qh_scr[h] = qh[h]
        fq_scr[h] = jnp.broadcast_to(fc_ref[0, :, h:h + 1], (TM, LANES))
    m_scr[...] = jnp.full(m_scr.shape, NEG, F32)
    l_scr[...] = jnp.zeros(l_scr.shape, F32)
    acc_scr[...] = jnp.zeros(acc_scr.shape, F32)

    def tile(start, frow, masked):
        k = k_ref[pl.ds(start, TM), :]
        v = v_ref[pl.ds(start, TM), :]
        for h in range(2):
            t = _dot_nt(qh_scr[h], k) - frow[h:h + 1, :]
            if masked:
                row = lax.broadcasted_iota(jnp.int32, (TM, TM), 0)
                col = lax.broadcasted_iota(jnp.int32, (TM, TM), 1)
                t = jnp.where(col <= row, t, NEG)
            m_old = m_scr[h]
            m_new = jnp.maximum(m_old, jnp.max(t, axis=1, keepdims=True) + fq_scr[h])
            shift = pltpu.repeat(fq_scr[h] - m_new, TM // LANES, axis=1)
            p = jnp.exp2(t + shift)
            alpha = jnp.exp2(m_old - m_new)
            l_scr[h] = alpha * l_scr[h] + jnp.sum(p, axis=1, keepdims=True)
            acc_scr[h] = alpha * acc_scr[h] + _dot(p.astype(BF16), v)
            m_scr[h] = m_new

    def body(ki, carry):
        tile(pl.multiple_of(ki * TM, TM), fr_ref[0, 0, ki], False)
        return carry

    lax.fori_loop(0, qi, body, 0)
    tile(pl.multiple_of(qi * TM, TM), fr_ref[0, 0, qi], True)
    o_pair = [acc_scr[h] / l_scr[h] for h in range(2)]
    o_ref[...] = _pair_norm_merge(o_pair, low, g_ref[...]).astype(BF16)


def _fox_prompt(qa, ka, va, f_col, f_row, gain, n_batch, seq):
    nq = seq // TM
    return pl.pallas_call(
        _fox_kernel,
        grid=(n_batch, N_PAIRS, nq),
        in_specs=[
            pl.BlockSpec((TM, LANES), lambda n, p, i: (n * nq + i, p)),
            pl.BlockSpec((seq, LANES), lambda n, p, i: (n, p)),
            pl.BlockSpec((seq, LANES), lambda n, p, i: (n, p)),
            pl.BlockSpec((1, TM, 2), lambda n, p, i: (p, n * nq + i, 0)),
            pl.BlockSpec((1, 1, nq, 2, TM), lambda n, p, i: (p, n, 0, 0, 0)),
            pl.BlockSpec((1, LANES), lambda n, p, i: (0, p)),
        ],
        out_specs=pl.BlockSpec((TM, LANES), lambda n, p, i: (n * nq + i, p)),
        out_shape=jax.ShapeDtypeStruct((n_batch * seq, GROUP_W), BF16),
        scratch_shapes=[pltpu.VMEM((2, TM, LANES), BF16)] + [pltpu.VMEM((2, TM, LANES), F32)] * 4,
        compiler_params=_cparams(3),
        name="fox_prompt",
    )(qa, ka, va, f_col, f_row, gain)


def _band_bias_kernel(r_ref, o_ref):
    x = jnp.broadcast_to(r_ref[0], (TM, 4 * TM))
    bias = pltpu.roll(x, 0, 1, stride=1, stride_axis=0)[:, :2 * TM] * LOG2E
    shift = CHUNK.bit_length() - 1
    ic = lax.broadcasted_iota(jnp.int32, (TM, 2 * TM), 0) >> shift
    jc = lax.broadcasted_iota(jnp.int32, (TM, 2 * TM), 1) >> shift
    visible = (jc >= ic) & (jc <= ic + N_PREV_CHUNKS)
    o_ref[0] = jnp.where(visible, bias, NEG)


def _band_bias_slab(rel_bias):
    n = rel_bias.shape[0]
    return pl.pallas_call(
        _band_bias_kernel,
        grid=(n,),
        in_specs=[pl.BlockSpec((1, 1, 4 * TM), lambda h: (h, 0, 0))],
        out_specs=pl.BlockSpec((1, TM, 2 * TM), lambda h: (h, 0, 0)),
        out_shape=jax.ShapeDtypeStruct((n, TM, 2 * TM), F32),
        compiler_params=_cparams(1),
        name="band_bias_slab",
    )(_band_bias_row(rel_bias))


def _band_kernel(q_ref, k_ref, v_ref, bm_ref, g_ref, o_ref):
    qi = pl.program_id(2)
    qh, low = _split_pair(q_ref[...])
    prev = pl.multiple_of(jnp.maximum(qi - 1, 0) * TM, TM)
    own = pl.multiple_of(qi * TM, TM)
    k0 = k_ref[pl.ds(prev, TM), :]
    v0 = v_ref[pl.ds(prev, TM), :]
    k1 = k_ref[pl.ds(own, TM), :]
    v1 = v_ref[pl.ds(own, TM), :]
    has_prev = qi > 0
    o_pair = []
    for h in range(2):
        s0 = jnp.where(has_prev, _dot_nt(qh[h], k0) + bm_ref[0, h, :, :TM], NEG)
        s1 = _dot_nt(qh[h], k1) + bm_ref[0, h, :, TM:]
        m = jnp.maximum(jnp.max(s0, axis=1, keepdims=True), jnp.max(s1, axis=1, keepdims=True))
        p0 = jnp.exp2(s0 - m)
        p1 = jnp.exp2(s1 - m)
        l = jnp.sum(p0, axis=1, keepdims=True) + jnp.sum(p1, axis=1, keepdims=True)
        o_pair.append((_dot(p0.astype(BF16), v0) + _dot(p1.astype(BF16), v1)) / l)
    o_ref[...] = _pair_norm_merge(o_pair, low, g_ref[...]).astype(BF16)


def _band_prompt(qb, kb, vb, bias_mask, gain, n_batch, seq):
    nq = seq // TM
    return pl.pallas_call(
        _band_kernel,
        grid=(N_PAIRS, n_batch, nq),
        in_specs=[
            pl.BlockSpec((TM, LANES), lambda p, n, i: (n * nq + i, p)),
            pl.BlockSpec((seq, LANES), lambda p, n, i: (n, p)),
            pl.BlockSpec((seq, LANES), lambda p, n, i: (n, p)),
            pl.BlockSpec((1, 2, TM, 2 * TM), lambda p, n, i: (p, 0, 0, 0)),
            pl.BlockSpec((1, LANES), lambda p, n, i: (0, p)),
        ],
        out_specs=pl.BlockSpec((TM, LANES), lambda p, n, i: (n * nq + i, p)),
        out_shape=jax.ShapeDtypeStruct((n_batch * seq, GROUP_W), BF16),
        compiler_params=_cparams(3),
        name="band_prompt",
    )(qb, kb, vb, bias_mask, gain)


def _two_part_attention(z_cache, z_new, v_cache, v_new):
    m = jnp.maximum(jnp.max(z_cache, axis=1, keepdims=True), jnp.max(z_new, axis=1, keepdims=True))
    p_c = jnp.exp2(z_cache - m)
    p_n = jnp.exp2(z_new - m)
    l = jnp.sum(p_c, axis=1, keepdims=True) + jnp.sum(p_n, axis=1, keepdims=True)
    return (_dot(p_c.astype(BF16), v_cache) + _dot(p_n.astype(BF16), v_new)) / l


def _sample_attn_kernel(qa_ref, ka_ref, va_ref, qb_ref, kb_ref, vb_ref,
                        cka_ref, cva_ref, ckb_ref, cvb_ref, fq_ref, fk_ref,
                        bm_ref, ga_ref, gb_ref, oa_ref, ob_ref):
    t_new = qa_ref.shape[0]
    past_a = cka_ref.shape[1]
    past_b = ckb_ref.shape[1]
    row = lax.broadcasted_iota(jnp.int32, (t_new, t_new), 0)
    col = lax.broadcasted_iota(jnp.int32, (t_new, t_new), 1)
    causal = col <= row
    for p in range(N_PAIRS):
        lanes = slice(p * LANES, (p + 1) * LANES)
        qh, low = _split_pair(qa_ref[:, lanes])
        k_c = cka_ref[0, :, lanes].astype(BF16)
        v_c = cva_ref[0, :, lanes].astype(BF16)
        k_n = ka_ref[:, lanes]
        v_n = va_ref[:, lanes]
        o_pair = []
        for j in range(2):
            h = 2 * p + j
            fq = fq_ref[0, :, h:h + 1]
            z_c = _dot_nt(qh[j], k_c) + fq - fk_ref[0, h:h + 1, :past_a]
            z_n = _dot_nt(qh[j], k_n) + fq - fk_ref[0, h:h + 1, past_a:past_a + t_new]
            z_n = jnp.where(causal, z_n, NEG)
            o_pair.append(_two_part_attention(z_c, z_n, v_c, v_n))
        oa_ref[:, lanes] = _pair_norm_merge(o_pair, low, ga_ref[:, lanes]).astype(BF16)
        qh, low = _split_pair(qb_ref[:, lanes])
        k_c = ckb_ref[0, :, lanes].astype(BF16)
        v_c = cvb_ref[0, :, lanes].astype(BF16)
        k_n = kb_ref[:, lanes]
        v_n = vb_ref[:, lanes]
        o_pair = []
        for j in range(2):
            h = 2 * p + j
            z_c = _dot_nt(qh[j], k_c) + bm_ref[h, :, :past_b]
            z_n = _dot_nt(qh[j], k_n) + bm_ref[h, :, past_b:past_b + t_new]
            o_pair.append(_two_part_attention(z_c, z_n, v_c, v_n))
        ob_ref[:, lanes] = _pair_norm_merge(o_pair, low, gb_ref[:, lanes]).astype(BF16)


def _sample_attention(proj, first_row_block, cka, cva, ckb, cvb, fq, fk, bias_slab, ga, gb):
    qa, ka, va, qb, kb, vb = proj
    n_b, past_a = cka.shape[0], cka.shape[1]
    past_b = ckb.shape[1]
    t_new = fq.shape[1]
    new_spec = pl.BlockSpec((t_new, GROUP_W), lambda b: (first_row_block + b, 0))
    full = lambda a: pl.BlockSpec(a.shape, lambda b: (0,) * a.ndim)
    out = jax.ShapeDtypeStruct((n_b * t_new, GROUP_W), BF16)
    return pl.pallas_call(
        _sample_attn_kernel,
        grid=(n_b,),
        in_specs=[new_spec] * 6 + [
            pl.BlockSpec((1, past_a, GROUP_W), lambda b: (b, 0, 0)),
            pl.BlockSpec((1, past_a, GROUP_W), lambda b: (b, 0, 0)),
            pl.BlockSpec((1, past_b, GROUP_W), lambda b: (b, 0, 0)),
            pl.BlockSpec((1, past_b, GROUP_W), lambda b: (b, 0, 0)),
            pl.BlockSpec((1, t_new, N_HEADS), lambda b: (b, 0, 0)),
            pl.BlockSpec((1, N_HEADS, fk.shape[2]), lambda b: (b, 0, 0)),
            pl.BlockSpec((N_HEADS, t_new, past_b + LANES), lambda b: (0, 0, 0)), full(ga), full(gb)],
        out_specs=[pl.BlockSpec((t_new, GROUP_W), lambda b: (b, 0))] * 2,
        out_shape=[out, out],
        compiler_params=_cparams(1),
        name="sample_attention",
    )(qa, ka, va, qb, kb, vb, cka, cva, ckb, cvb, fq, fk, bias_slab, ga, gb)


def _route(logits):
    lane = lax.broadcasted_iota(jnp.int32, logits.shape, 1).astype(F32)
    far = float(4 * LANES)
    is_group = (lane >= N_EXPERTS) & (lane < N_EXPERTS + N_GROUPS)
    gl = jnp.where(is_group, logits, NEG)
    g_max = jnp.max(gl, axis=1, keepdims=True)
    g_idx = jnp.min(jnp.where(gl == g_max, lane, far), axis=1, keepdims=True) - N_EXPERTS
    g_sum = jnp.sum(jnp.where(is_group, jnp.exp(gl - g_max), 0.0), axis=1, keepdims=True)
    g_w = 1.0 / g_sum
    first = g_idx * EXPERTS_PER_GROUP
    in_group = (lane >= first) & (lane < first + EXPERTS_PER_GROUP)
    el = jnp.where(in_group, logits, NEG)
    v1 = jnp.max(el, axis=1, keepdims=True)
    i1 = jnp.min(jnp.where(el == v1, lane, far), axis=1, keepdims=True)
    el2 = jnp.where(lane == i1, NEG, el)
    v2 = jnp.max(el2, axis=1, keepdims=True)
    i2 = jnp.min(jnp.where(el2 == v2, lane, far), axis=1, keepdims=True)
    e2 = jnp.exp(v2 - v1)
    w1 = g_w / (1.0 + e2)
    w2 = g_w * e2 / (1.0 + e2)
    return jnp.where(lane == i1, w1, 0.0) + jnp.where(lane == i2, w2, 0.0)


def _merge_kernel(oa_ref, ob_ref, x_ref, g1_ref, sh_ref, sc_ref, wo_ref, nf_ref,
                  wrh_ref, wrl_ref, br_ref, x1_ref, h2_ref, cmb_ref, hf_scr):
    y = _dot(oa_ref[...], wo_ref[:GROUP_W, :]) + _dot(ob_ref[...], wo_ref[GROUP_W:, :])
    gain = nf_ref[...]
    for r in range(MODS_PER_TILE):
        rows = pl.ds(r * MOD_ROWS, MOD_ROWS)
        x1 = x_ref[rows, :] + g1_ref[r:r + 1, :] * y[r * MOD_ROWS:(r + 1) * MOD_ROWS, :]
        x1_ref[rows, :] = x1
        h2 = _modulated_norm(x1, gain, sc_ref[r:r + 1, :], sh_ref[r:r + 1, :])
        hf_scr[rows, :] = h2
        h2_ref[rows, :] = h2.astype(BF16)
    h = hf_scr[...]
    h_hi = h.astype(BF16)
    h_lo = (h - h_hi.astype(F32)).astype(BF16)
    logits = (_dot(h_hi, wrh_ref[...]) + _dot(h_lo, wrh_ref[...])
              + _dot(h_hi, wrl_ref[...]) + br_ref[...])
    cmb_ref[...] = _route(logits)


def _merge_and_route(oa, ob, x, gate_g, shift_g, scale_g, wo, norm_ffn, wr_hi, wr_lo, br):
    t = x.shape[0]
    row_spec = lambda w: pl.BlockSpec((TM, w), lambda i: (i, 0))
    mod_spec = pl.BlockSpec((MODS_PER_TILE, D_MODEL), lambda i: (i, 0))
    const = lambda shape: pl.BlockSpec(shape, lambda i: (0, 0))
    return pl.pallas_call(
        _merge_kernel,
        grid=(t // TM,),
        in_specs=[row_spec(GROUP_W), row_spec(GROUP_W), row_spec(D_MODEL),
                  mod_spec, mod_spec, mod_spec, const((D_MODEL, D_MODEL)), const((1, D_MODEL)),
                  const((D_MODEL, ROUTER_PAD)), const((D_MODEL, ROUTER_PAD)), const((1, ROUTER_PAD))],
        out_specs=[row_spec(D_MODEL), row_spec(D_MODEL), row_spec(ROUTER_PAD)],
        out_shape=[jax.ShapeDtypeStruct((t, D_MODEL), F32),
                   jax.ShapeDtypeStruct((t, D_MODEL), BF16),
                   jax.ShapeDtypeStruct((t, ROUTER_PAD), F32)],
        scratch_shapes=[pltpu.VMEM((TM, D_MODEL), F32)],
        compiler_params=_cparams(1),
        name="merge_and_route",
    )(oa, ob, x, gate_g, shift_g, scale_g, wo, norm_ffn, wr_hi, wr_lo, br)


def _moe_kernel(h_ref, wg_ref, wu_ref, wd_ref, cmb_ref, x1_ref, g2_ref, o_ref, acc_scr):
    e = pl.program_id(1)

    @pl.when(e == 0)
    def _():
        acc_scr[...] = jnp.zeros(acc_scr.shape, F32)

    h = h_ref[...]
    lane = lax.broadcasted_iota(jnp.int32, cmb_ref.shape, 1)
    cw = jnp.sum(jnp.where(lane == e, cmb_ref[...], 0.0), axis=1, keepdims=True)
    hh = _silu(_dot(h, wg_ref[0])) * _dot(h, wu_ref[0]) * cw
    acc_scr[...] += _dot(hh.astype(BF16), wd_ref[0])

    @pl.when(e == N_EXPERTS - 1)
    def _():
        for r in range(MODS_PER_TILE):
            rows = pl.ds(r * MOD_ROWS, MOD_ROWS)
            o_ref[rows, :] = x1_ref[rows, :] + g2_ref[r:r + 1, :] * acc_scr[rows, :]


def _moe(h2, wg, wu, wd, cmb, x1, gate_g):
    t = h2.shape[0]
    return pl.pallas_call(
        _moe_kernel,
        grid=(t // TM, N_EXPERTS),
        in_specs=[
            pl.BlockSpec((TM, D_MODEL), lambda i, e: (i, 0)),
            pl.BlockSpec((1, D_MODEL, D_EXPERT), lambda i, e: (e, 0, 0)),
            pl.BlockSpec((1, D_MODEL, D_EXPERT), lambda i, e: (e, 0, 0)),
            pl.BlockSpec((1, D_EXPERT, D_MODEL), lambda i, e: (e, 0, 0)),
            pl.BlockSpec((TM, ROUTER_PAD), lambda i, e: (i, 0)),
            pl.BlockSpec((TM, D_MODEL), lambda i, e: (i, 0)),
            pl.BlockSpec((MODS_PER_TILE, D_MODEL), lambda i, e: (i, 0)),
        ],
        out_specs=pl.BlockSpec((TM, D_MODEL), lambda i, e: (i, 0)),
        out_shape=jax.ShapeDtypeStruct((t, D_MODEL), F32),
        scratch_shapes=[pltpu.VMEM((TM, D_MODEL), F32)],
        compiler_params=_cparams(2),
        name="moe_experts",
    )(h2, wg, wu, wd, cmb, x1, gate_g)


def _final_norm_kernel(x_ref, g_ref, o_ref):
    x = x_ref[...]
    ms = jnp.mean(x * x, axis=-1, keepdims=True)
    o_ref[...] = x * lax.rsqrt(ms + RMS_EPS) * g_ref[...]


def _final_norm(x, gain):
    t = x.shape[0]
    return pl.pallas_call(
        _final_norm_kernel,
        grid=(t // TM,),
        in_specs=[pl.BlockSpec((TM, D_MODEL), lambda i: (i, 0)),
                  pl.BlockSpec((1, D_MODEL), lambda i: (0, 0))],
        out_specs=pl.BlockSpec((TM, D_MODEL), lambda i: (i, 0)),
        out_shape=jax.ShapeDtypeStruct((t, D_MODEL), F32),
        compiler_params=_cparams(1),
        name="final_norm",
    )(x, gain)


def _band_bias_row(rel_bias):
    n = rel_bias.shape[0]
    far_past = jnp.broadcast_to(rel_bias[:, 2 * REL_CLIP:], (n, TM - REL_CLIP + 1))
    ramp = rel_bias[:, 2 * REL_CLIP - 1:0:-1]
    ahead = jnp.broadcast_to(rel_bias[:, :1], (n, TM - REL_CLIP))
    wrap = jnp.broadcast_to(rel_bias[:, 2 * REL_CLIP:], (n, 2 * TM))
    row = jnp.concatenate([far_past, ramp, ahead, wrap], axis=1)
    return row.reshape(n, 1, 4 * TM)


def _pack_w_in(w_in, b_fgate):
    splits = (GROUP_W, 2 * GROUP_W, 3 * GROUP_W, 3 * GROUP_W + N_HEADS,
              4 * GROUP_W + N_HEADS, 5 * GROUP_W + N_HEADS)
    qa, ka, va, fa, qb, kb, vb = jnp.split(w_in, splits, axis=-1)
    fa = jnp.pad(fa, ((0, 0), (0, FGATE_PAD - N_HEADS)))
    w = jnp.concatenate([qa, ka, va, qb, kb, vb, fa], axis=-1).astype(BF16)
    bf = jnp.pad(b_fgate, (0, FGATE_PAD - N_HEADS)).reshape(1, FGATE_PAD)
    return w, bf


def _pack_router(w_rg, b_rg, w_re, b_re):
    pad = ROUTER_PAD - N_EXPERTS - N_GROUPS
    w = jnp.pad(jnp.concatenate([w_re, w_rg], axis=-1), ((0, 0), (0, pad)))
    b = jnp.pad(jnp.concatenate([b_re, b_rg]), (0, pad)).reshape(1, ROUTER_PAD)
    w_hi = w.astype(BF16)
    w_lo = (w - w_hi.astype(F32)).astype(BF16)
    return w_hi, w_lo, b


def kernel(x_prompt, x_sample, cache_k_a, cache_v_a, cache_logf_a, cache_k_b, cache_v_b, c_prompt, c_sample, w_ada, b_ada, norm_mix, norm_ffn, w_in, b_fgate, rel_bias, onorm_a, onorm_b, w_out, w_router_group, b_router_group, w_router_expert, b_router_expert, w_gate, w_up, w_down, norm_final):
    n_p, seq, _ = x_prompt.shape
    n_s, t_new, _ = x_sample.shape
    depth = w_in.shape[0]
    past_a = cache_k_a.shape[2]
    past_b = cache_k_b.shape[2]
    t_p = n_p * seq
    t_s = n_s * t_new
    nq = seq // TM
    assert seq % TM == 0 and t_s % TM == 0 and TM % t_new == 0 and t_new == MOD_ROWS == CHUNK
    assert past_b == N_PREV_CHUNKS * CHUNK and seq >= past_b

    x = jnp.concatenate([x_prompt.reshape(t_p, D_MODEL), x_sample.reshape(t_s, D_MODEL)], axis=0)

    n_cond = n_p + n_s
    cond_rows = -(-n_cond // 8) * 8
    c_all = jnp.pad(jnp.concatenate([c_prompt, c_sample], axis=0), ((0, cond_rows - n_cond), (0, 0)))
    mod = _ada_modulation(c_all, w_ada, b_ada)

    def per_group(m):
        rep = jnp.broadcast_to(m[:n_p, None, :], (n_p, seq // MOD_ROWS, D_MODEL))
        return jnp.concatenate([rep.reshape(-1, D_MODEL), m[n_p:n_cond]], axis=0)

    pad_len = -(-(past_a + t_new) // LANES) * LANES
    st_p, st_s = [], []
    for l in range(depth):
        sh1, sc1, g1, sh2, sc2, g2 = [per_group(m) for m in jnp.split(mod[l], N_MOD, axis=-1)]
        w_pack, bf_pad = _pack_w_in(w_in[l], b_fgate[l])
        (qa, ka, va, qb, kb, vb, kaf, vaf, kbf, vbf, logf_pad) = _in_projection(
            x, sh1, sc1, norm_mix[l].reshape(1, D_MODEL), w_pack, bf_pad)
        logf = logf_pad[:, :N_HEADS]

        lf_p = logf[:t_p].reshape(n_p, seq, N_HEADS).transpose(0, 2, 1).reshape(n_p * N_HEADS, seq)
        f_p = _cumsum_lanes(lf_p, N_HEADS, seq).reshape(n_p, N_PAIRS, 2, nq, TM)
        f_row = f_p.transpose(1, 0, 3, 2, 4)
        f_col = f_p.transpose(1, 0, 3, 4, 2).reshape(N_PAIRS, t_p, 2)
        lf_s = jnp.concatenate([cache_logf_a[l].astype(F32), logf[t_p:].reshape(n_s, t_new, N_HEADS)], axis=1)
        lf_s = jnp.pad(lf_s.transpose(0, 2, 1), ((0, 0), (0, 0), (0, pad_len - past_a - t_new)))
        f_s = _cumsum_lanes(lf_s.reshape(n_s * N_HEADS, pad_len), n_s * N_HEADS, pad_len)
        f_s = f_s.reshape(n_s, N_HEADS, pad_len)
        fq_s = f_s[:, :, past_a:past_a + t_new].transpose(0, 2, 1)

        ga = onorm_a[l].reshape(1, GROUP_W)
        gb = onorm_b[l].reshape(1, GROUP_W)
        oa_p = _fox_prompt(qa, ka, va, f_col, f_row, ga, n_p, seq)
        bias_slab = _band_bias_slab(rel_bias[l])
        ob_p = _band_prompt(qb, kb, vb, bias_slab.reshape(N_PAIRS, 2, TM, 2 * TM), gb, n_p, seq)
        oa_s, ob_s = _sample_attention(
            (qa, ka, va, qb, kb, vb), t_p // t_new,
            cache_k_a[l].reshape(n_s, past_a, GROUP_W), cache_v_a[l].reshape(n_s, past_a, GROUP_W),
            cache_k_b[l].reshape(n_s, past_b, GROUP_W), cache_v_b[l].reshape(n_s, past_b, GROUP_W),
            fq_s, f_s, bias_slab, ga, gb)
        oa = jnp.concatenate([oa_p, oa_s], axis=0)
        ob = jnp.concatenate([ob_p, ob_s], axis=0)

        wr_hi, wr_lo, br = _pack_router(w_router_group[l], b_router_group[l],
                                        w_router_expert[l], b_router_expert[l])
        x1, h2, cmb = _merge_and_route(oa, ob, x, g1, sh2, sc2, w_out[l].astype(BF16),
                                       norm_ffn[l].reshape(1, D_MODEL), wr_hi, wr_lo, br)
        x = _moe(h2, w_gate[l].astype(BF16), w_up[l].astype(BF16), w_down[l].astype(BF16), cmb, x1, g2)

        heads = lambda a, n, rows: a.reshape(n, rows, N_HEADS, HEAD_DIM)
        st_p.append((heads(kaf[:t_p], n_p, seq), heads(vaf[:t_p], n_p, seq),
                     logf[:t_p].reshape(n_p, seq, N_HEADS),
                     heads(kbf[:t_p], n_p, seq)[:, seq - past_b:], heads(vbf[:t_p], n_p, seq)[:, seq - past_b:]))
        st_s.append((heads(kaf[t_p:], n_s, t_new), heads(vaf[t_p:], n_s, t_new),
                     logf[t_p:].reshape(n_s, t_new, N_HEADS),
                     heads(kbf[t_p:], n_s, t_new), heads(vbf[t_p:], n_s, t_new)))

    y = _final_norm(x, norm_final.reshape(1, D_MODEL))
    stack = lambda sts, i: jnp.stack([st[i] for st in sts], axis=0)
    return (y[:t_p].reshape(n_p, seq, D_MODEL), y[t_p:].reshape(n_s, t_new, D_MODEL),
            stack(st_p, 0), stack(st_p, 1), stack(st_p, 2), stack(st_p, 3), stack(st_p, 4),
            stack(st_s, 0), stack(st_s, 1), stack(st_s, 2), stack(st_s, 3), stack(st_s, 4))
```
